```python
import jax, jax.numpy as jnp
from jax import lax
import numpy as np

D_MODEL = 1024
BATCH = 4
SEQ = 4096
DEPTH = 2
DEC_BATCH = 32
DEC_SEQ = 64
PAST_LEN = 2048

CHUNK = 64
N_BACK = 8
BAND = N_BACK * CHUNK
A_HEADS = 4
A_DH = D_MODEL // 8
A_W = A_HEADS * A_DH
B_HEADS = 8
B_DH = D_MODEL // 16
B_W = B_HEADS * B_DH
REL_CLIP = 128
N_REL = CHUNK + REL_CLIP
C_HEADS = 16
C_DH = D_MODEL // 16
C_W = C_HEADS * C_DH
IN_AB = 4 * A_W + 2 * A_HEADS + 3 * B_W
IN_FOX = 3 * C_W + C_HEADS
FFN_HIDDEN = -(-8 * D_MODEL // (3 * 256)) * 256
N_AB_LAYERS = (DEPTH + 1) // 2
N_FOX_LAYERS = DEPTH // 2
Q_BLOCK = 128
RMS_EPS = 1e-6
NEG_INF = -1e30

kernel_name = 'streaming_mlstm_band_fox_encoder_step'


def rmsnorm(x, g):
    xf = x.astype(jnp.float32)
    y = xf * lax.rsqrt(jnp.mean(xf * xf, axis=-1, keepdims=True) + RMS_EPS)
    return (y * g.astype(jnp.float32)).astype(x.dtype)


def swiglu(x, w_in, w_out):
    g, u = jnp.split(x @ w_in, 2, axis=-1)
    return (jax.nn.silu(g) * u) @ w_out


def to_heads(x, n_heads):
    b, t, _ = x.shape
    return x.reshape(b, t, n_heads, -1).transpose(0, 2, 1, 3)


def from_heads(x):
    b, h, t, d = x.shape
    return x.transpose(0, 2, 1, 3).reshape(b, t, h * d)


def head_layernorm(h, g):
    mu = jnp.mean(h, axis=-1, keepdims=True)
    var = jnp.mean(jnp.square(h - mu), axis=-1, keepdims=True)
    return from_heads((h - mu) * lax.rsqrt(var + RMS_EPS)) * g.astype(jnp.float32)


def mlstm_chunk(carry, inp):
    C, n, m = carry
    q, k, v, ig, lf = inp
    L = q.shape[2]
    b = jnp.cumsum(lf, axis=-1)
    causal = jnp.tril(jnp.ones((L, L), dtype=bool))
    D = jnp.where(causal, b[..., :, None] - b[..., None, :] + ig[..., None, :], NEG_INF)
    inter = b + m[..., None]
    m_t = jnp.maximum(inter, jnp.max(D, axis=-1))
    w_intra = jnp.exp(D - m_t[..., None]) * jnp.einsum('bhtd,bhsd->bhts', q, k)
    w_inter = jnp.exp(inter - m_t)
    num = (w_inter[..., None] * jnp.einsum('bhtd,bhde->bhte', q, C)
           + jnp.einsum('bhts,bhse->bhte', w_intra, v))
    den = w_inter * jnp.einsum('bhtd,bhd->bht', q, n) + jnp.sum(w_intra, axis=-1)
    h = num / jnp.maximum(jnp.abs(den), jnp.exp(-m_t))[..., None]
    b_last = b[..., -1]
    w_src = b_last[..., None] - b + ig
    m_new = jnp.maximum(b_last + m, jnp.max(w_src, axis=-1))
    decay = jnp.exp(b_last + m - m_new)
    w_s = jnp.exp(w_src - m_new[..., None])
    C_new = decay[..., None, None] * C + jnp.einsum('bhs,bhsd,bhse->bhde', w_s, k, v)
    n_new = decay[..., None] * n + jnp.einsum('bhs,bhsd->bhd', w_s, k)
    return (C_new, n_new, m_new), h


def mlstm_seq(q, k, v, ig, lf, C0, n0, m0):
    bsz, nh, t_len, _ = q.shape
    L = min(CHUNK, t_len)
    nck = t_len // L

    def chunks(a):
        return jnp.moveaxis(a.reshape(a.shape[:2] + (nck, L) + a.shape[3:]), 2, 0)

    (C, n, m), hs = lax.scan(mlstm_chunk, (C0, n0, m0),
                             (chunks(q), chunks(k), chunks(v), chunks(ig), chunks(lf)))
    h = jnp.moveaxis(hs, 0, 2).reshape(bsz, nh, t_len, -1)
    return h, (C, n, m)


def rel_bias(table, rel):
    idx = jnp.clip(rel, -(CHUNK - 1), REL_CLIP) + (CHUNK - 1)
    return table[:, idx].astype(jnp.float32)


def band_attn_prompt(q, k, v, table):
    bsz, nh, t_len, dh = q.shape
    nck = t_len // CHUNK
    idx = jnp.arange(nck)[:, None] + jnp.arange(N_BACK + 1)[None, :]

    def band(a):
        ap = jnp.pad(a.reshape(bsz, nh, nck, CHUNK, dh),
                     ((0, 0), (0, 0), (N_BACK, 0), (0, 0), (0, 0)))
        return jnp.take(ap, idx, axis=2).reshape(bsz, nh, nck, (N_BACK + 1) * CHUNK, dh)

    qc = q.reshape(bsz, nh, nck, CHUNK, dh).astype(jnp.float32)
    kb = band(k).astype(jnp.float32)
    vb = band(v).astype(jnp.float32)
    valid = jnp.repeat(idx >= N_BACK, CHUNK, axis=1)
    rel = N_BACK * CHUNK + jnp.arange(CHUNK)[:, None] - jnp.arange((N_BACK + 1) * CHUNK)[None, :]
    s = jnp.einsum('bhcqd,bhckd->bhcqk', qc, kb) * dh ** -0.5 + rel_bias(table, rel)[:, None]
    s = jnp.where(valid[:, None, :], s, NEG_INF)
    p = jax.nn.softmax(s, axis=-1)
    return jnp.einsum('bhcqk,bhckd->bhcqd', p, vb).reshape(bsz, nh, t_len, dh)


def band_attn_sample(q, k, v, ck, cv, table):
    w = ck.shape[2]
    t_len, dh = q.shape[2], q.shape[3]
    kk = jnp.concatenate([ck.astype(k.dtype), k], axis=2).astype(jnp.float32)
    vv = jnp.concatenate([cv.astype(v.dtype), v], axis=2).astype(jnp.float32)
    k_off = jnp.concatenate([jnp.arange(w) - w, jnp.arange(t_len)])
    rel = jnp.arange(t_len)[:, None] - k_off[None, :]
    s = jnp.einsum('bhqd,bhkd->bhqk', q.astype(jnp.float32), kk) * dh ** -0.5 + rel_bias(table, rel)
    p = jax.nn.softmax(s, axis=-1)
    return jnp.einsum('bhqk,bhkd->bhqd', p, vv)


def fox_attend(q, k, v, Fq, Fk, q_pos, k_pos):
    s = jnp.einsum('bhqd,bhkd->bhqk', q.astype(jnp.float32), k.astype(jnp.float32)) * C_DH ** -0.5
    s = s + Fq[..., :, None] - Fk[..., None, :]
    s = jnp.where(k_pos[None, :] <= q_pos[:, None], s, NEG_INF)
    p = jax.nn.softmax(s, axis=-1)
    return jnp.einsum('bhqk,bhkd->bhqd', p, v.astype(jnp.float32))


def fox_attn_prompt(q, k, v, lf):
    bsz, nh, t_len, dh = q.shape
    F = jnp.cumsum(lf, axis=-1)
    pos = jnp.arange(t_len)
    nb = t_len // Q_BLOCK
    qb = q.reshape(bsz, nh, nb, Q_BLOCK, dh).transpose(2, 0, 1, 3, 4)
    Fb = F.reshape(bsz, nh, nb, Q_BLOCK).transpose(2, 0, 1, 3)
    pb = pos.reshape(nb, Q_BLOCK)
    o = lax.map(lambda a: fox_attend(a[0], k, v, a[1], F, a[2], pos), (qb, Fb, pb))
    return o.transpose(1, 2, 0, 3, 4).reshape(bsz, nh, t_len, dh)


def fox_attn_sample(q, k, v, lf, ck, cv, clf):
    p_len = ck.shape[2]
    t_len = q.shape[2]
    kk = jnp.concatenate([ck.astype(k.dtype), k], axis=2)
    vv = jnp.concatenate([cv.astype(v.dtype), v], axis=2)
    F = jnp.cumsum(jnp.concatenate([clf.astype(jnp.float32), lf], axis=-1), axis=-1)
    pos = jnp.arange(p_len + t_len)
    return fox_attend(q, kk, vv, F[..., p_len:], F, pos[p_len:], pos)


def mixer_ab(h, w_in, b_gate, gain, table, w_out, mstate, band_cache):
    bsz = h.shape[0]
    splits = [A_W, 2 * A_W, 3 * A_W, 4 * A_W, 4 * A_W + 2 * A_HEADS,
              4 * A_W + 2 * A_HEADS + B_W, 4 * A_W + 2 * A_HEADS + 2 * B_W]
    qa, ka, va, oa, gates, qb, kb, vb = jnp.split(h @ w_in, splits, axis=-1)
    gates = gates.astype(jnp.float32) + b_gate.astype(jnp.float32)
    ig = gates[..., :A_HEADS].transpose(0, 2, 1)
    lf = jax.nn.log_sigmoid(gates[..., A_HEADS:]).transpose(0, 2, 1)
    q = to_heads(qa, A_HEADS).astype(jnp.float32)
    k = to_heads(ka, A_HEADS).astype(jnp.float32) * A_DH ** -0.5
    v = to_heads(va, A_HEADS).astype(jnp.float32)
    if mstate is None:
        C0 = jnp.zeros((bsz, A_HEADS, A_DH, A_DH), jnp.float32)
        n0 = jnp.zeros((bsz, A_HEADS, A_DH), jnp.float32)
        m0 = jnp.zeros((bsz, A_HEADS), jnp.float32)
    else:
        C0, n0, m0 = (s.astype(jnp.float32) for s in mstate)
    hA, mst = mlstm_seq(q, k, v, ig, lf, C0, n0, m0)
    outA = (head_layernorm(hA, gain) * jax.nn.sigmoid(oa.astype(jnp.float32))).astype(h.dtype)
    qB, kB, vB = to_heads(qb, B_HEADS), to_heads(kb, B_HEADS), to_heads(vb, B_HEADS)
    if band_cache is None:
        oB = band_attn_prompt(qB, kB, vB, table)
        t_len = h.shape[1]
        w = min(BAND, t_len)
        rows = (kB[:, :, t_len - w:], vB[:, :, t_len - w:])
    else:
        oB = band_attn_sample(qB, kB, vB, band_cache[0], band_cache[1], table)
        rows = (kB, vB)
    out = jnp.concatenate([outA, from_heads(oB).astype(h.dtype)], axis=-1) @ w_out
    return out, mst, rows


def mixer_fox(h, w_in, b_f, w_out, cache):
    q, k, v, fp = jnp.split(h @ w_in, [C_W, 2 * C_W, 3 * C_W], axis=-1)
    lf = jax.nn.log_sigmoid(fp.astype(jnp.float32) + b_f.astype(jnp.float32)).transpose(0, 2, 1)
    q, k, v = to_heads(q, C_HEADS), to_heads(k, C_HEADS), to_heads(v, C_HEADS)
    if cache is None:
        o = fox_attn_prompt(q, k, v, lf)
    else:
        o = fox_attn_sample(q, k, v, lf, cache[0], cache[1], cache[2])
    return from_heads(o).astype(h.dtype) @ w_out, (k, v, lf)


def trunk(x, caches, norm_mix, norm_ffn, norm_final, w_in_ab, b_gate_ab, mlstm_gain,
          rel_bias_table, w_out_ab, w_in_fox, b_fox_f, w_out_fox, w_ffn_in, w_ffn_out):
    ab_states, fox_states = [], []
    for layer in range(DEPTH):
        h = rmsnorm(x, norm_mix[layer])
        j = layer // 2
        if layer % 2 == 0:
            ms = None if caches is None else (caches[0][j], caches[1][j], caches[2][j])
            bc = None if caches is None else (caches[3][j], caches[4][j])
            out, mst, rows = mixer_ab(h, w_in_ab[j], b_gate_ab[j], mlstm_gain[j],
                                      rel_bias_table[j], w_out_ab[j], ms, bc)
            ab_states.append(mst + rows)
        else:
            fc = None if caches is None else (caches[5][j], caches[6][j], caches[7][j])
            out, rows = mixer_fox(h, w_in_fox[j], b_fox_f[j], w_out_fox[j], fc)
            fox_states.append(rows)
        x = x + out
        x = x + swiglu(rmsnorm(x, norm_ffn[layer]), w_ffn_in[layer], w_ffn_out[layer])
    y = rmsnorm(x, norm_final)
    st_ab = [jnp.stack(s, axis=0) for s in zip(*ab_states)]
    st_fox = [jnp.stack(s, axis=0) for s in zip(*fox_states)]
    return y, st_ab, st_fox


def setup_inputs(seed: int = 0) -> dict:
    key = jax.random.key(seed)
    ks = jax.random.split(key, 26)

    def nrm(k, shape, scale):
        return jax.random.normal(k, shape, jnp.float32) * scale

    w_band = min(BAND, PAST_LEN)
    b_gate_ab = jnp.concatenate(
        [nrm(ks[20], (N_AB_LAYERS, A_HEADS), 0.1),
         jnp.linspace(3.0, 6.0, A_HEADS)[None, :] + nrm(ks[21], (N_AB_LAYERS, A_HEADS), 0.1)], axis=-1)
    return {
        'x_prompt': nrm(ks[0], (BATCH, SEQ, D_MODEL), 1.0),
        'x_sample': nrm(ks[1], (DEC_BATCH, DEC_SEQ, D_MODEL), 1.0),
        'state_mlstm_C': nrm(ks[2], (N_AB_LAYERS, DEC_BATCH, A_HEADS, A_DH, A_DH), 0.5),
        'state_mlstm_n': nrm(ks[3], (N_AB_LAYERS, DEC_BATCH, A_HEADS, A_DH), 1.0),
        'state_mlstm_m': nrm(ks[4], (N_AB_LAYERS, DEC_BATCH, A_HEADS), 0.5),
        'cache_band_k': nrm(ks[5], (N_AB_LAYERS, DEC_BATCH, B_HEADS, w_band, B_DH), 1.0),
        'cache_band_v': nrm(ks[6], (N_AB_LAYERS, DEC_BATCH, B_HEADS, w_band, B_DH), 1.0),
        'cache_fox_k': nrm(ks[7], (N_FOX_LAYERS, DEC_BATCH, C_HEADS, PAST_LEN, C_DH), 1.0),
        'cache_fox_v': nrm(ks[8], (N_FOX_LAYERS, DEC_BATCH, C_HEADS, PAST_LEN, C_DH), 1.0),
        'cache_fox_logf': jax.nn.log_sigmoid(
            2.0 + nrm(ks[9], (N_FOX_LAYERS, DEC_BATCH, C_HEADS, PAST_LEN), 1.0)),
        'norm_mix': 1.0 + nrm(ks[10], (DEPTH, D_MODEL), 0.05),
        'norm_ffn': 1.0 + nrm(ks[11], (DEPTH, D_MODEL), 0.05),
        'norm_final': 1.0 + nrm(ks[12], (D_MODEL,), 0.05),
        'w_in_ab': nrm(ks[13], (N_AB_LAYERS, D_MODEL, IN_AB), D_MODEL ** -0.5),
        'b_gate_ab': b_gate_ab,
        'mlstm_gain': 1.0 + nrm(ks[14], (N_AB_LAYERS, A_W), 0.05),
        'rel_bias_table': nrm(ks[15], (N_AB_LAYERS, B_HEADS, N_REL), 0.2),
        'w_out_ab': nrm(ks[16], (N_AB_LAYERS, A_W + B_W, D_MODEL), (A_W + B_W) ** -0.5),
        'w_in_fox': nrm(ks[17], (N_FOX_LAYERS, D_MODEL, IN_FOX), D_MODEL ** -0.5),
        'b_fox_f': jnp.linspace(0.0, 4.0, C_HEADS)[None, :] + nrm(ks[18], (N_FOX_LAYERS, C_HEADS), 0.1),
        'w_out_fox': nrm(ks[19], (N_FOX_LAYERS, C_W, D_MODEL), C_W ** -0.5),
        'w_ffn_in': nrm(ks[22], (DEPTH, D_MODEL, 2 * FFN_HIDDEN), D_MODEL ** -0.5),
        'w_ffn_out': nrm(ks[23], (DEPTH, FFN_HIDDEN, D_MODEL), FFN_HIDDEN ** -0.5),
    }


def reference(x_prompt, x_sample, state_mlstm_C, state_mlstm_n, state_mlstm_m,
              cache_band_k, cache_band_v, cache_fox_k, cache_fox_v, cache_fox_logf,
              norm_mix, norm_ffn, norm_final, w_in_ab, b_gate_ab, mlstm_gain,
              rel_bias_table, w_out_ab, w_in_fox, b_fox_f, w_out_fox, w_ffn_in, w_ffn_out):
    y_prompt, (p_C, p_n, p_m, p_bk, p_bv), (p_fk, p_fv, p_flf) = trunk(
        x_prompt, None, norm_mix, norm_ffn, norm_final, w_in_ab, b_gate_ab, mlstm_gain,
        rel_bias_table, w_out_ab, w_in_fox, b_fox_f, w_out_fox, w_ffn_in, w_ffn_out)
    caches = (state_mlstm_C, state_mlstm_n, state_mlstm_m, cache_band_k, cache_band_v,
              cache_fox_k, cache_fox_v, cache_fox_logf)
    y_sample, (s_C, s_n, s_m, s_bk, s_bv), (s_fk, s_fv, s_flf) = trunk(
        x_sample, caches, norm_mix, norm_ffn, norm_final, w_in_ab, b_gate_ab, mlstm_gain,
        rel_bias_table, w_out_ab, w_in_fox, b_fox_f, w_out_fox, w_ffn_in, w_ffn_out)
    return (y_prompt, y_sample, p_C, p_n, p_m, p_bk, p_bv, p_fk, p_fv, p_flf,
            s_C, s_n, s_m, s_bk, s_bv, s_fk, s_fv, s_flf)
```

```python
import functools

import jax
import jax.numpy as jnp
import numpy as np
from jax import lax
from jax.experimental import pallas as pl
from jax.experimental.pallas import tpu as pltpu

F32 = jnp.float32
BF16 = jnp.bfloat16

RMS_EPS = 1e-6
NEG_INF = -1e30
CHUNK = 64
N_BACK = 8
BAND = N_BACK * CHUNK
REL_CLIP = 128
A_HEADS, A_DH = 4, 128
B_HEADS, B_DH = 8, 64
C_HEADS, C_DH = 16, 64
A_W = A_HEADS * A_DH
B_W = B_HEADS * B_DH
C_W = C_HEADS * C_DH
LANES = 128
V7X_VMEM_LIMIT = 56 * 1024 * 1024

NT_DIMS = (((1,), (1,)), ((), ()))
TN_DIMS = (((0,), (0,)), ((), ()))


def _params(*sem):
    return pltpu.CompilerParams(dimension_semantics=sem, vmem_limit_bytes=V7X_VMEM_LIMIT)


def _tile(n, pref):
    t = min(n, pref)
    assert n % t == 0, (n, t)
    return t


def _rms(x, g):
    return x * lax.rsqrt(jnp.mean(x * x, axis=-1, keepdims=True) + RMS_EPS) * g


def _log_sigmoid(z):
    return jnp.minimum(z, 0.0) - jnp.log1p(jnp.exp(-jnp.abs(z)))


def _sigmoid(z):
    return 1.0 / (1.0 + jnp.exp(-z))


def _proj_body(*refs, n_w, scales, head_layout, ls_start):
    x_ref, g_ref = refs[0], refs[1]
    w_refs = refs[2:2 + n_w]
    wg_ref, bg_ref = refs[2 + n_w], refs[3 + n_w]
    out_refs = refs[4 + n_w:4 + 2 * n_w]
    gate_ref = refs[4 + 2 * n_w]
    hn_ref = refs[5 + 2 * n_w]

    @pl.when(pl.program_id(1) == 0)
    def _():
        hn_ref[...] = _rms(x_ref[...], g_ref[...]).astype(BF16)
        z = jnp.dot(hn_ref[...], wg_ref[...], preferred_element_type=F32) + bg_ref[...]
        col = lax.broadcasted_iota(jnp.int32, z.shape, 1)
        gate_ref[...] = jnp.where(col >= ls_start, _log_sigmoid(z), z)

    hn = hn_ref[...]
    for i in range(n_w):
        acc = jnp.dot(hn, w_refs[i][...], preferred_element_type=F32)
        if scales[i] != 1.0:
            acc = acc * scales[i]
        o = out_refs[i]
        if head_layout[i]:
            nb, nh, rows, dh = o.shape
            for bb in range(nb):
                for hh in range(nh):
                    o[bb, hh] = acc[bb * rows:(bb + 1) * rows, hh * dh:(hh + 1) * dh].astype(o.dtype)
        else:
            o[...] = acc.astype(o.dtype)


def _proj(x, g, weights, scales, dtypes, head_layout, w_gate, b_gate, ls_start, *, seq, tm, tn, dh=64):
    m, d = x.shape
    nw = weights[0].shape[1]
    tm = _tile(m, tm)
    tn = _tile(nw, tn)
    n_w = len(weights)
    in_specs = [pl.BlockSpec((tm, d), lambda i, j: (i, 0)), pl.BlockSpec((1, d), lambda i, j: (0, 0))]
    in_specs += [pl.BlockSpec((d, tn), lambda i, j: (0, j)) for _ in weights]
    in_specs += [pl.BlockSpec((d, LANES), lambda i, j: (0, 0)), pl.BlockSpec((1, LANES), lambda i, j: (0, 0))]
    out_shapes, out_specs = [], []
    for dt, hl in zip(dtypes, head_layout):
        if hl:
            out_shapes.append(jax.ShapeDtypeStruct((m // seq, nw // dh, seq, dh), dt))
            if tm <= seq:
                per = seq // tm
                out_specs.append(pl.BlockSpec((1, tn // dh, tm, dh), lambda i, j: (i // per, j, i % per, 0)))
            else:
                out_specs.append(pl.BlockSpec((tm // seq, tn // dh, seq, dh), lambda i, j: (i, j, 0, 0)))
        else:
            out_shapes.append(jax.ShapeDtypeStruct((m, nw), dt))
            out_specs.append(pl.BlockSpec((tm, tn), lambda i, j: (i, j)))
    out_shapes.append(jax.ShapeDtypeStruct((m, LANES), F32))
    out_specs.append(pl.BlockSpec((tm, LANES), lambda i, j: (i, 0)))
    body = functools.partial(_proj_body, n_w=n_w, scales=tuple(scales), head_layout=tuple(head_layout),
                             ls_start=ls_start)
    return pl.pallas_call(
        body, grid=(m // tm, nw // tn), in_specs=in_specs, out_specs=out_specs, out_shape=out_shapes,
        scratch_shapes=[pltpu.VMEM((tm, d), BF16)],
        compiler_params=_params("parallel", "arbitrary"), name="norm_proj",
    )(x, g.reshape(1, d), *weights, w_gate, b_gate)


def _mlstm_body(q_ref, k_ref, v_ref, oa_ref, gc_ref, gr_ref, gain_ref, c0_ref, n0_ref, m0_ref,
                out_ref, cn_ref, nn_ref, mn_ref, c_s, n_s, m_s, *, n_chunks):
    t = pl.program_id(1)

    @pl.when(t == 0)
    def _():
        c_s[...] = c0_ref[0]
        n_s[0:A_HEADS, :] = n0_ref[0]
        for h in range(A_HEADS):
            m_s[h:h + 1, :] = jnp.broadcast_to(m0_ref[0][:, h:h + 1], (1, LANES))

    row = lax.broadcasted_iota(jnp.int32, (CHUNK, CHUNK), 0)
    col = lax.broadcasted_iota(jnp.int32, (CHUNK, CHUNK), 1)
    tril = row >= col

    def chunk(c, carry):
        r0 = pl.multiple_of(c * CHUNK, CHUNK)
        rows = pl.ds(r0, CHUNK)
        gr = gr_ref[c]
        gc = gc_ref[rows, :]
        for h in range(A_HEADS):
            lanes = slice(h * A_DH, (h + 1) * A_DH)
            q = q_ref[rows, lanes]
            k = k_ref[rows, lanes]
            v = v_ref[rows, lanes]
            ig_row, lf_row = gr[h:h + 1, :], gr[A_HEADS + h:A_HEADS + h + 1, :]
            ig_col, lf_col = gc[:, h:h + 1], gc[:, A_HEADS + h:A_HEADS + h + 1]
            m_prev = m_s[h:h + 1, 0:1]
            n_prev = n_s[h:h + 1, :]
            c_prev = c_s[h]
            b_col = jnp.sum(jnp.where(tril, lf_row, 0.0), axis=1, keepdims=True)
            b_row = jnp.sum(jnp.where(row <= col, lf_col, 0.0), axis=0, keepdims=True)
            b_last = jnp.sum(lf_row, axis=1, keepdims=True)
            d = jnp.where(tril, b_col - b_row + ig_row, NEG_INF)
            inter = b_col + m_prev
            m_t = jnp.maximum(inter, jnp.max(d, axis=1, keepdims=True))
            s = lax.dot_general(q, k, NT_DIMS, preferred_element_type=F32)
            w_intra = jnp.exp(d - m_t) * s
            w_inter = jnp.exp(inter - m_t)
            num = (w_inter * jnp.dot(q, c_prev.astype(BF16), preferred_element_type=F32)
                   + jnp.dot(w_intra.astype(BF16), v, preferred_element_type=F32))
            den = (w_inter * jnp.sum(q.astype(F32) * n_prev, axis=1, keepdims=True)
                   + jnp.sum(w_intra, axis=1, keepdims=True))
            hid = num / jnp.maximum(jnp.abs(den), jnp.exp(-m_t))
            w_src_row = b_last - b_row + ig_row
            w_src_col = b_last - b_col + ig_col
            m_new = jnp.maximum(b_last + m_prev, jnp.max(w_src_row, axis=1, keepdims=True))
            decay = jnp.exp(b_last + m_prev - m_new)
            kw = k.astype(F32) * jnp.exp(w_src_col - m_new)
            c_s[h] = decay * c_prev + lax.dot_general(kw.astype(BF16), v, TN_DIMS, preferred_element_type=F32)
            n_s[h:h + 1, :] = decay * n_prev + jnp.sum(kw, axis=0, keepdims=True)
            m_s[h:h + 1, :] = jnp.broadcast_to(m_new, (1, LANES))
            mu = jnp.mean(hid, axis=1, keepdims=True)
            cen = hid - mu
            var = jnp.mean(cen * cen, axis=1, keepdims=True)
            y = cen * lax.rsqrt(var + RMS_EPS) * gain_ref[:, lanes] * _sigmoid(oa_ref[rows, lanes])
            out_ref[rows, lanes] = y.astype(out_ref.dtype)
        return carry

    lax.fori_loop(0, n_chunks, chunk, 0)

    @pl.when(t == pl.num_programs(1) - 1)
    def _():
        cn_ref[0] = c_s[...]
        nn_ref[0] = n_s[0:A_HEADS, :]
        for h in range(A_HEADS):
            mn_ref[0, :, h:h + 1] = m_s[h:h + 1, 0:1]


def _mlstm(q, k, v, oa, gates, gain, c0, n0, m0, *, bsz, seq, tb):
    m = q.shape[0]
    tb = _tile(seq, tb)
    per = seq // tb
    gates_row = gates[:, :2 * A_HEADS].reshape(m // CHUNK, CHUNK, 2 * A_HEADS).transpose(0, 2, 1)
    tok = lambda b, t: (b * per + t, 0)
    st4 = lambda b, t: (b, 0, 0, 0)
    st3 = lambda b, t: (b, 0, 0)
    in_specs = [pl.BlockSpec((tb, A_W), tok)] * 4 + [
        pl.BlockSpec((tb, LANES), tok),
        pl.BlockSpec((tb // CHUNK, 2 * A_HEADS, CHUNK), lambda b, t: (b * per + t, 0, 0)),
        pl.BlockSpec((1, A_W), lambda b, t: (0, 0)),
        pl.BlockSpec((1, A_HEADS, A_DH, A_DH), st4),
        pl.BlockSpec((1, A_HEADS, A_DH), st3),
        pl.BlockSpec((1, 1, A_HEADS), st3),
    ]
    out_shape = [jax.ShapeDtypeStruct((m, A_W), BF16),
                 jax.ShapeDtypeStruct((bsz, A_HEADS, A_DH, A_DH), F32),
                 jax.ShapeDtypeStruct((bsz, A_HEADS, A_DH), F32),
                 jax.ShapeDtypeStruct((bsz, 1, A_HEADS), F32)]
    out_specs = [pl.BlockSpec((tb, A_W), tok),
                 pl.BlockSpec((1, A_HEADS, A_DH, A_DH), st4),
                 pl.BlockSpec((1, A_HEADS, A_DH), st3),
                 pl.BlockSpec((1, 1, A_HEADS), st3)]
    out, cn, nn, mn = pl.pallas_call(
        functools.partial(_mlstm_body, n_chunks=tb // CHUNK),
        grid=(bsz, per), in_specs=in_specs, out_specs=out_specs, out_shape=out_shape,
        scratch_shapes=[pltpu.VMEM((A_HEADS, A_DH, A_DH), F32), pltpu.VMEM((8, LANES), F32),
                        pltpu.VMEM((8, LANES), F32)],
        compiler_params=_params("parallel", "arbitrary"), name="mlstm",
    )(q, k, v, oa, gates, gates_row, gain.reshape(1, A_W), c0, n0, m0.reshape(bsz, 1, A_HEADS))
    return out, cn, nn, mn.reshape(bsz, A_HEADS)


def _softmax_pv(parts):
    mx = functools.reduce(jnp.maximum, [jnp.max(s, axis=1, keepdims=True) for s, _ in parts])
    acc, den = 0.0, 0.0
    for s, v in parts:
        p = jnp.exp(s - mx)
        den = den + jnp.sum(p, axis=1, keepdims=True)
        acc = acc + jnp.dot(p.astype(BF16), v, preferred_element_type=F32)
    return acc / den


def _band_prompt_body(q_ref, kp_ref, kc_ref, vp_ref, vc_ref, bias_ref, o_ref, pk_ref, pv_ref, kcat, vcat):
    i = pl.program_id(2)
    tq = q_ref.shape[0]
    win = tq + CHUNK
    kcat[0:tq] = kp_ref[...].astype(BF16)
    kcat[tq:] = kc_ref[...].astype(BF16)
    vcat[0:tq] = vp_ref[...].astype(BF16)
    vcat[tq:] = vc_ref[...].astype(BF16)
    col = lax.broadcasted_iota(jnp.int32, (CHUNK, win), 1)
    first = i == 0
    for j in range(tq // CHUNK):
        rows = slice(j * CHUNK, (j + 1) * CHUNK)
        keys = slice(j * CHUNK, j * CHUNK + win)
        for hh in range(2):
            lanes = slice(hh * B_DH, (hh + 1) * B_DH)
            s = lax.dot_general(q_ref[rows, lanes], kcat[keys, lanes], NT_DIMS,
                                preferred_element_type=F32) + bias_ref[hh]
            s = jnp.where(jnp.logical_and(first, col < tq - j * CHUNK), NEG_INF, s)
            o_ref[rows, lanes] = _softmax_pv([(s, vcat[keys, lanes])]).astype(o_ref.dtype)

    @pl.when(i == pl.num_programs(2) - 1)
    def _():
        for hh in range(2):
            lanes = slice(hh * B_DH, (hh + 1) * B_DH)
            pk_ref[0, hh] = kc_ref[:, lanes]
            pv_ref[0, hh] = vc_ref[:, lanes]


def _band_prompt(q, k, v, bias, *, bsz, seq):
    m = q.shape[0]
    assert seq % BAND == 0
    per = seq // BAND
    cur = lambda b, p, i: (b * per + i, p)
    prev = lambda b, p, i: (b * per + jnp.maximum(i - 1, 0), p)
    blk = (BAND, 2 * B_DH)
    in_specs = [pl.BlockSpec(blk, cur), pl.BlockSpec(blk, prev), pl.BlockSpec(blk, cur),
                pl.BlockSpec(blk, prev), pl.BlockSpec(blk, cur),
                pl.BlockSpec((2, CHUNK, BAND + CHUNK), lambda b, p, i: (p, 0, 0))]
    rows_spec = pl.BlockSpec((1, 2, BAND, B_DH), lambda b, p, i: (b, p, 0, 0))
    rows_shape = jax.ShapeDtypeStruct((bsz, B_HEADS, BAND, B_DH), F32)
    return pl.pallas_call(
        _band_prompt_body, grid=(bsz, B_HEADS // 2, per), in_specs=in_specs,
        out_specs=[pl.BlockSpec(blk, cur), rows_spec, rows_spec],
        out_shape=[jax.ShapeDtypeStruct((m, B_W), BF16), rows_shape, rows_shape],
        scratch_shapes=[pltpu.VMEM((2 * BAND, 2 * B_DH), BF16), pltpu.VMEM((2 * BAND, 2 * B_DH), BF16)],
        compiler_params=_params("parallel", "parallel", "arbitrary"), name="band_prompt",
    )(q, k, k, v, v, bias)


def _band_sample_body(q_ref, k_ref, v_ref, ck_ref, cv_ref, bias_ref, o_ref, sk_ref, sv_ref):
    w = ck_ref.shape[3]
    for h in range(B_HEADS):
        lanes = slice(h * B_DH, (h + 1) * B_DH)
        q = q_ref[:, lanes]
        k_new, v_new = k_ref[:, lanes], v_ref[:, lanes]
        sk_ref[0, h] = k_new
        sv_ref[0, h] = v_new
        bias = bias_ref[h]
        s_old = lax.dot_general(q, ck_ref[0, 0, h].astype(BF16), NT_DIMS, preferred_element_type=F32) + bias[:, :w]
        s_new = lax.dot_general(q, k_new.astype(BF16), NT_DIMS, preferred_element_type=F32) + bias[:, w:]
        o = _softmax_pv([(s_old, cv_ref[0, 0, h].astype(BF16)), (s_new, v_new.astype(BF16))])
        o_ref[:, lanes] = o.astype(o_ref.dtype)


def _band_sample(q, k, v, ck, cv, bias, *, bsz, seq):
    m = q.shape[0]
    w = ck.shape[3]
    tok = pl.BlockSpec((seq, B_W), lambda b: (b, 0))
    cache = pl.BlockSpec((1, 1, B_HEADS, w, B_DH), lambda b: (0, b, 0, 0, 0))
    rows_spec = pl.BlockSpec((1, B_HEADS, seq, B_DH), lambda b: (b, 0, 0, 0))
    rows_shape = jax.ShapeDtypeStruct((bsz, B_HEADS, seq, B_DH), F32)
    return pl.pallas_call(
        _band_sample_body, grid=(bsz,),
        in_specs=[tok, tok, tok, cache, cache, pl.BlockSpec((B_HEADS, seq, w + seq), lambda b: (0, 0, 0))],
        out_specs=[tok, rows_spec, rows_spec],
        out_shape=[jax.ShapeDtypeStruct((m, B_W), BF16), rows_shape, rows_shape],
        compiler_params=_params("parallel"), name="band_sample",
    )(q, k, v, ck, cv, bias)


def _linres_body(*refs, n_in):
    x_ref, o_ref = refs[0], refs[1 + 2 * n_in]
    acc = x_ref[...]
    for i in range(n_in):
        acc = acc + jnp.dot(refs[1 + i][...], refs[1 + n_in + i][...], preferred_element_type=F32)
    o_ref[...] = acc


def _linres(x, acts, weights, *, tm):
    m, d = x.shape
    tm = _tile(m, tm)
    n_in = len(acts)
    row = lambda i: (i, 0)
    in_specs = [pl.BlockSpec((tm, d), row)]
    in_specs += [pl.BlockSpec((tm, a.shape[1]), row) for a in acts]
    in_specs += [pl.BlockSpec(w.shape, lambda i: (0, 0)) for w in weights]
    return pl.pallas_call(
        functools.partial(_linres_body, n_in=n_in), grid=(m // tm,), in_specs=in_specs,
        out_specs=pl.BlockSpec((tm, d), row), out_shape=jax.ShapeDtypeStruct((m, d), F32),
        compiler_params=_params("parallel"), name="out_proj",
    )(x, *acts, *weights)


def _ffn_body(x_ref, g_ref, wg_ref, wu_ref, wo_ref, gf_ref, o_ref, hn_s, acc_s, *, final_norm):
    j = pl.program_id(1)

    @pl.when(j == 0)
    def _():
        hn_s[...] = _rms(x_ref[...], g_ref[...]).astype(BF16)
        acc_s[...] = jnp.zeros_like(acc_s)

    hn = hn_s[...]
    gate = jnp.dot(hn, wg_ref[...], preferred_element_type=F32)
    up = jnp.dot(hn, wu_ref[...], preferred_element_type=F32)
    act = gate * _sigmoid(gate) * up
    acc_s[...] += jnp.dot(act.astype(BF16), wo_ref[...], preferred_element_type=F32)

    @pl.when(j == pl.num_programs(1) - 1)
    def _():
        y = x_ref[...] + acc_s[...]
        if final_norm:
            y = _rms(y, gf_ref[...])
        o_ref[...] = y


def _ffn(x, g, w_in, w_out, g_final, *, final_norm, tm, th):
    m, d = x.shape
    hidden = w_out.shape[0]
    tm = _tile(m, tm)
    th = _tile(hidden, th)
    nh = hidden // th
    row = lambda i, j: (i, 0)
    vec = pl.BlockSpec((1, d), lambda i, j: (0, 0))
    return pl.pallas_call(
        functools.partial(_ffn_body, final_norm=final_norm), grid=(m // tm, nh),
        in_specs=[pl.BlockSpec((tm, d), row), vec,
                  pl.BlockSpec((d, th), lambda i, j: (0, j)),
                  pl.BlockSpec((d, th), lambda i, j: (0, j + nh)),
                  pl.BlockSpec((th, d), lambda i, j: (j, 0)), vec],
        out_specs=pl.BlockSpec((tm, d), row), out_shape=jax.ShapeDtypeStruct((m, d), F32),
        scratch_shapes=[pltpu.VMEM((tm, d), BF16), pltpu.VMEM((tm, d), F32)],
        compiler_params=_params("parallel", "arbitrary"), name="ffn",
    )(x, g.reshape(1, d), w_in, w_in, w_out, g_final.reshape(1, d))


def _cumsum_body(x_ref, o_ref):
    rows, n = x_ref.shape
    r = lax.broadcasted_iota(jnp.int32, (LANES, LANES), 0)
    c = lax.broadcasted_iota(jnp.int32, (LANES, LANES), 1)
    upper = (r <= c).astype(BF16)
    carry = jnp.zeros((rows, 1), F32)
    for j in range(n // LANES):
        lanes = slice(j * LANES, (j + 1) * LANES)
        blk = x_ref[:, lanes]
        hi = blk.astype(BF16)
        rem = blk - hi.astype(F32)
        mid = rem.astype(BF16)
        lo = (rem - mid.astype(F32)).astype(BF16)
        y = (jnp.dot(hi, upper, preferred_element_type=F32) + jnp.dot(mid, upper, preferred_element_type=F32)
             + jnp.dot(lo, upper, preferred_element_type=F32)) + carry
        o_ref[:, lanes] = y
        carry = y[:, LANES - 1:LANES]


def _cumsum_lanes(x):
    return pl.pallas_call(
        _cumsum_body, grid=(1,), in_specs=[pl.BlockSpec(x.shape, lambda i: (0, 0))],
        out_specs=pl.BlockSpec(x.shape, lambda i: (0, 0)), out_shape=jax.ShapeDtypeStruct(x.shape, F32),
        compiler_params=_params("arbitrary"), name="cumsum",
    )(x)


def _fox_prompt_body(qi_tab, ki_tab, q_ref, k_ref, v_ref, fq_ref, fk_ref, o_ref, m_s, l_s, acc_s):
    step = pl.program_id(2)
    qi, ki = qi_tab[step], ki_tab[step]
    tq, tk = q_ref.shape[2], k_ref.shape[2]

    @pl.when(ki == 0)
    def _():
        m_s[...] = jnp.full_like(m_s, NEG_INF)
        l_s[...] = jnp.zeros_like(l_s)
        acc_s[...] = jnp.zeros_like(acc_s)

    def attend(diagonal):
        for hh in range(2):
            s = lax.dot_general(q_ref[0, hh], k_ref[0, hh].astype(BF16), NT_DIMS, preferred_element_type=F32)
            t = s - fk_ref[0, 0, hh:hh + 1, :]
            if diagonal:
                row = lax.broadcasted_iota(jnp.int32, (tq, tk), 0)
                col = lax.broadcasted_iota(jnp.int32, (tq, tk), 1)
                t = jnp.where(col <= row, t, NEG_INF)
            fq = fq_ref[0, 0][:, hh:hh + 1]
            m_prev = m_s[hh]
            m_new = jnp.maximum(m_prev, jnp.max(t, axis=1, keepdims=True) + fq)
            alpha = jnp.exp(m_prev - m_new)
            p = jnp.exp(t + (fq - m_new))
            l_s[hh] = alpha * l_s[hh] + jnp.sum(p, axis=1, keepdims=True)
            acc_s[hh] = alpha * acc_s[hh] + jnp.dot(p.astype(BF16), v_ref[0, hh].astype(BF16),
                                                    preferred_element_type=F32)
            m_s[hh] = m_new

    @pl.when(ki != qi)
    def _():
        attend(False)

    @pl.when(ki == qi)
    def _():
        attend(True)
        for hh in range(2):
            o_ref[:, hh * C_DH:(hh + 1) * C_DH] = (acc_s[hh] / l_s[hh]).astype(o_ref.dtype)


def _fox_prompt(q, k, v, f_rows, *, bsz, seq, tq):
    tq = _tile(seq, tq)
    nq = seq // tq
    pairs = [(a, b) for a in range(nq) for b in range(a + 1)]
    qi_tab = jnp.asarray(np.array([p[0] for p in pairs], np.int32))
    ki_tab = jnp.asarray(np.array([p[1] for p in pairs], np.int32))
    fk = f_rows.reshape(bsz, C_HEADS // 2, 2, seq)
    fq = fk.transpose(0, 1, 3, 2)
    grid_spec = pltpu.PrefetchScalarGridSpec(
        num_scalar_prefetch=2, grid=(bsz, C_HEADS // 2, len(pairs)),
        in_specs=[pl.BlockSpec((1, 2, tq, C_DH), lambda b, p, s, qt, kt: (b, p, qt[s], 0)),
                  pl.BlockSpec((1, 2, tq, C_DH), lambda b, p, s, qt, kt: (b, p, kt[s], 0)),
                  pl.BlockSpec((1, 2, tq, C_DH), lambda b, p, s, qt, kt: (b, p, kt[s], 0)),
                  pl.BlockSpec((1, 1, tq, 2), lambda b, p, s, qt, kt: (b, p, qt[s], 0)),
                  pl.BlockSpec((1, 1, 2, tq), lambda b, p, s, qt, kt: (b, p, 0, kt[s]))],
        out_specs=pl.BlockSpec((tq, 2 * C_DH), lambda b, p, s, qt, kt: (b * nq + qt[s], p)),
        scratch_shapes=[pltpu.VMEM((2, tq, 1), F32), pltpu.VMEM((2, tq, 1), F32),
                        pltpu.VMEM((2, tq, C_DH), F32)])
    return pl.pallas_call(
        _fox_prompt_body, grid_spec=grid_spec, out_shape=jax.ShapeDtypeStruct((bsz * seq, C_W), BF16),
        compiler_params=_params("parallel", "parallel", "arbitrary"), name="fox_prompt",
    )(qi_tab, ki_tab, q, k, v, fq, fk)


def _fox_sample_body(q_ref, k_ref, v_ref, ck_ref, cv_ref, fq_ref, fk_ref, o_ref):
    past = ck_ref.shape[3]
    seq = q_ref.shape[2]
    row = lax.broadcasted_iota(jnp.int32, (seq, seq), 0)
    col = lax.broadcasted_iota(jnp.int32, (seq, seq), 1)
    for hh in range(2):
        q = q_ref[0, hh]
        fq = fq_ref[0, 0][:, hh:hh + 1]
        fk = fk_ref[0, 0, hh:hh + 1, :]
        s_old = lax.dot_general(q, ck_ref[0, 0, hh].astype(BF16), NT_DIMS, preferred_element_type=F32)
        s_new = lax.dot_general(q, k_ref[0, hh].astype(BF16), NT_DIMS, preferred_element_type=F32)
        t_old = s_old + fq - fk[:, :past]
        t_new = jnp.where(col <= row, s_new + fq - fk[:, past:past + seq], NEG_INF)
        o = _softmax_pv([(t_old, cv_ref[0, 0, hh].astype(BF16)), (t_new, v_ref[0, hh].astype(BF16))])
        o_ref[:, hh * C_DH:(hh + 1) * C_DH] = o.astype(o_ref.dtype)


def _fox_sample(q, k, v, ck, cv, f_rows, *, bsz, seq):
    past = ck.shape[3]
    width = f_rows.shape[1]
    fk = f_rows.reshape(bsz, C_HEADS // 2, 2, width)
    fq = fk[..., past:past + seq].transpose(0, 1, 3, 2)
    new = pl.BlockSpec((1, 2, seq, C_DH), lambda b, p: (b, p, 0, 0))
    cache = pl.BlockSpec((1, 1, 2, past, C_DH), lambda b, p: (0, b, p, 0, 0))
    return pl.pallas_call(
        _fox_sample_body, grid=(bsz, C_HEADS // 2),
        in_specs=[new, new, new, cache, cache,
                  pl.BlockSpec((1, 1, seq, 2), lambda b, p: (b, p, 0, 0)),
                  pl.BlockSpec((1, 1, 2, width), lambda b, p: (b, p, 0, 0))],
        out_specs=pl.BlockSpec((seq, 2 * C_DH), lambda b, p: (b, p)),
        out_shape=jax.ShapeDtypeStruct((bsz * seq, C_W), BF16),
        compiler_params=_params("parallel", "parallel"), name="fox_sample",
    )(q, k, v, ck, cv, fq, fk)


def _pad_lanes(a):
    return jnp.pad(a, ((0, 0), (0, LANES - a.shape[1])))


def _rel_bias(table, rel):
    idx = jnp.clip(rel, -(CHUNK - 1), REL_CLIP) + (CHUNK - 1)
    return table[:, idx].astype(F32)


def _trunk(x, caches, norm_mix, norm_ffn, norm_final, w_in_ab, b_gate_ab, mlstm_gain, rel_bias_table,
           w_out_ab, w_in_fox, b_fox_f, w_out_fox, w_ffn_in, w_ffn_out):
    bsz, seq, d = x.shape
    x = x.reshape(bsz * seq, d)
    sample = caches is not None

    w = w_in_ab[0]
    w_main = [w[:, i * A_W:(i + 1) * A_W].astype(BF16) for i in range(4)]
    g0 = 4 * A_W + 2 * A_HEADS
    w_main += [w[:, g0 + i * B_W:g0 + (i + 1) * B_W].astype(BF16) for i in range(3)]
    w_gate = _pad_lanes(w[:, 4 * A_W:g0]).astype(BF16)
    b_gate = _pad_lanes(b_gate_ab[0].astype(F32).reshape(1, -1))
    qa, ka, va, oa, qb, kb, vb, gates = _proj(
        x, norm_mix[0], w_main, [1.0, A_DH ** -0.5, 1.0, 1.0, B_DH ** -0.5, 1.0, 1.0],
        [BF16, BF16, BF16, F32, BF16, F32, F32], [False] * 7, w_gate, b_gate, A_HEADS,
        seq=seq, tm=512, tn=A_W)
    if sample:
        c0, n0, m0 = caches[0][0], caches[1][0], caches[2][0]
    else:
        c0 = jnp.zeros((bsz, A_HEADS, A_DH, A_DH), F32)
        n0 = jnp.zeros((bsz, A_HEADS, A_DH), F32)
        m0 = jnp.zeros((bsz, A_HEADS), F32)
    out_a, c_n, n_n, m_n = _mlstm(qa, ka, va, oa, gates, mlstm_gain[0], c0, n0, m0, bsz=bsz, seq=seq, tb=512)
    if sample:
        wlen = caches[3].shape[3]
        rel = jnp.arange(seq)[:, None] + wlen - jnp.arange(wlen + seq)[None, :]
        out_b, band_k, band_v = _band_sample(qb, kb, vb, caches[3], caches[4], _rel_bias(rel_bias_table[0], rel),
                                             bsz=bsz, seq=seq)
    else:
        rel = BAND + jnp.arange(CHUNK)[:, None] - jnp.arange(BAND + CHUNK)[None, :]
        out_b, band_k, band_v = _band_prompt(qb, kb, vb, _rel_bias(rel_bias_table[0], rel), bsz=bsz, seq=seq)
    wo = w_out_ab[0].astype(BF16)
    x = _linres(x, [out_a, out_b], [wo[:A_W], wo[A_W:]], tm=512)
    x = _ffn(x, norm_ffn[0], w_ffn_in[0].astype(BF16), w_ffn_out[0].astype(BF16), norm_final,
             final_norm=False, tm=1024, th=256)

    w = w_in_fox[0]
    w_qkv = [w[:, i * C_W:(i + 1) * C_W].astype(BF16) for i in range(3)]
    w_f = _pad_lanes(w[:, 3 * C_W:]).astype(BF16)
    b_f = _pad_lanes(b_fox_f[0].astype(F32).reshape(1, -1))
    qf, kf, vf, lf = _proj(x, norm_mix[1], w_qkv, [C_DH ** -0.5, 1.0, 1.0], [BF16, F32, F32], [True] * 3,
                           w_f, b_f, 0, seq=seq, tm=256, tn=C_W)
    logf = lf[:, :C_HEADS].reshape(bsz, seq, C_HEADS).transpose(0, 2, 1)
    if sample:
        past = caches[7].shape[3]
        width = -(-(past + seq) // LANES) * LANES
        lf_all = jnp.concatenate([caches[7][0].astype(F32), logf], axis=-1)
        lf_all = jnp.pad(lf_all, ((0, 0), (0, 0), (0, width - past - seq)))
        f_rows = _cumsum_lanes(lf_all.reshape(bsz * C_HEADS, width))
        o = _fox_sample(qf, kf, vf, caches[5], caches[6], f_rows, bsz=bsz, seq=seq)
    else:
        f_rows = _cumsum_lanes(logf.reshape(bsz * C_HEADS, seq))
        o = _fox_prompt(qf, kf, vf, f_rows, bsz=bsz, seq=seq, tq=512)
    x = _linres(x, [o], [w_out_fox[0].astype(BF16)], tm=512)
    y = _ffn(x, norm_ffn[1], w_ffn_in[1].astype(BF16), w_ffn_out[1].astype(BF16), norm_final,
             final_norm=True, tm=1024, th=256)
    ab = (c_n[None], n_n[None], m_n[None], band_k[None], band_v[None])
    fox = (kf[None], vf[None], logf[None])
    return y.reshape(bsz, seq, d), ab, fox


def kernel(x_prompt, x_sample, state_mlstm_C, state_mlstm_n, state_mlstm_m, cache_band_k, cache_band_v,
           cache_fox_k, cache_fox_v, cache_fox_logf, norm_mix, norm_ffn, norm_final, w_in_ab, b_gate_ab,
           mlstm_gain, rel_bias_table, w_out_ab, w_in_fox, b_fox_f, w_out_fox, w_ffn_in, w_ffn_out):
    weights = (norm_mix, norm_ffn, norm_final, w_in_ab, b_gate_ab, mlstm_gain, rel_bias_table, w_out_ab,
               w_in_fox, b_fox_f, w_out_fox, w_ffn_in, w_ffn_out)
    y_p, ab_p, fox_p = _trunk(x_prompt, None, *weights)
    caches = (state_mlstm_C, state_mlstm_n, state_mlstm_m, cache_band_k, cache_band_v,
              cache_fox_k, cache_fox_v, cache_fox_logf)
    y_s, ab_s, fox_s = _trunk(x_sample, caches, *weights)
    return (y_p, y_s) + ab_p + fox_p + ab_s + fox_s
```
